```python
import jax, jax.numpy as jnp
from jax import lax
import numpy as np

D_MODEL = 2048
BATCH = 2
SEQ = 16384
DEPTH = 1

D_MIX = D_MODEL
GDN_DK = 128
GDN_DV = 128
GDN_HEADS = (D_MIX // 2) // GDN_DV
GDN_CHUNK = 64
CONV_K = 5
HGRN_DK = 128
HGRN_DV = 128
HGRN_HEADS = (D_MIX - GDN_HEADS * GDN_DV) // HGRN_DV
HGRN_CHUNK = 32
D_FF = -(-8 * D_MODEL // (3 * 256)) * 256
NORM_EPS = 1e-6

GA_QK = GDN_HEADS * GDN_DK
GA_V = GDN_HEADS * GDN_DV
GDN_CONV_CH = 2 * GA_QK + GA_V
HB_K = HGRN_HEADS * HGRN_DK
HB_V = HGRN_HEADS * HGRN_DV
SPLIT_SIZES = (GDN_CONV_CH, GA_V, 2 * GDN_HEADS, 2 * GDN_HEADS, HB_K, 2 * HB_K, HB_V, HB_V)
D_IN = GDN_CONV_CH + GA_V + 4 * GDN_HEADS + 3 * HB_K + 2 * HB_V

kernel_name = "hybrid_gdn_hgrn2_parallel_bidir_block"


def _rmsnorm(x, w):
    xf = x.astype(jnp.float32)
    y = xf * lax.rsqrt(jnp.mean(xf * xf, axis=-1, keepdims=True) + NORM_EPS)
    return (y * w.astype(jnp.float32)).astype(x.dtype)


def _l2norm(x):
    return x * lax.rsqrt(jnp.sum(x * x, axis=-1, keepdims=True) + NORM_EPS)


def _split_cols(t):
    outs, off = [], 0
    for size in SPLIT_SIZES:
        outs.append(t[..., off:off + size])
        off += size
    return outs


def _heads(t, n_heads):
    b, s, _ = t.shape
    return t.reshape(b, s, n_heads, -1).transpose(0, 2, 1, 3)


def _merge_heads(t):
    b, h, s, d = t.shape
    return t.transpose(0, 2, 1, 3).reshape(b, s, h * d)


def _dir_params(t, n_heads):
    b, s, _ = t.shape
    return t.reshape(b, s, 2, n_heads).transpose(2, 0, 3, 1)


def _short_conv(x, w):
    c = x.shape[-1]
    return lax.conv_general_dilated(
        x, w[:, None, :].astype(x.dtype), window_strides=(1,),
        padding=[((CONV_K - 1) // 2, CONV_K // 2)],
        dimension_numbers=("NWC", "WIO", "NWC"), feature_group_count=c)


def _gdn_scan(q, k, v, g, beta):
    b_, h_, s_, dk = q.shape
    dv = v.shape[-1]
    c = GDN_CHUNK
    n = s_ // c
    q = q.reshape(b_, h_, n, c, dk)
    k = k.reshape(b_, h_, n, c, dk)
    v = v.reshape(b_, h_, n, c, dv)
    g = g.reshape(b_, h_, n, c)
    beta = beta.reshape(b_, h_, n, c)
    G = jnp.cumsum(g, axis=-1)
    incl = jnp.tril(jnp.ones((c, c), dtype=bool))
    strict = jnp.tril(jnp.ones((c, c), dtype=bool), -1)
    decay = jnp.exp(jnp.where(incl, G[..., :, None] - G[..., None, :], -jnp.inf))
    L = beta[..., None] * jnp.einsum("bhnrd,bhnjd->bhnrj", k, k) * jnp.where(strict, decay, 0.0)
    gamma = jnp.exp(G)
    rhs = jnp.concatenate([beta[..., None] * v, (beta * gamma)[..., None] * k], axis=-1)
    sol = lax.linalg.triangular_solve(jnp.eye(c, dtype=q.dtype) + L, rhs,
                                      left_side=True, lower=True, unit_diagonal=True)
    U, Wk = sol[..., :dv], sol[..., dv:]
    Aqk = jnp.einsum("bhnrd,bhnjd->bhnrj", q, k) * decay
    q_dec = q * gamma[..., None]
    k_dec = k * jnp.exp(G[..., -1:] - G)[..., None]
    g_last = jnp.exp(G[..., -1])

    def step(S0, xs):
        u_c, wk_c, a_c, qd_c, kd_c, gl_c = xs
        w = u_c - jnp.einsum("bhrk,bhkv->bhrv", wk_c, S0)
        o = jnp.einsum("bhrk,bhkv->bhrv", qd_c, S0) + jnp.einsum("bhrj,bhjv->bhrv", a_c, w)
        S1 = gl_c[..., None, None] * S0 + jnp.einsum("bhjk,bhjv->bhkv", kd_c, w)
        return S1, o

    xs = tuple(jnp.moveaxis(t, 2, 0) for t in (U, Wk, Aqk, q_dec, k_dec, g_last))
    S0 = jnp.zeros((b_, h_, dk, dv), q.dtype)
    _, o = lax.scan(step, S0, xs)
    return jnp.moveaxis(o, 0, 2).reshape(b_, h_, s_, dv)


def _hgrn2_scan(q, v, k, logf):
    b_, h_, s_, dk = q.shape
    dv = v.shape[-1]
    c = HGRN_CHUNK
    n = s_ // c
    to_chunks = lambda t: jnp.moveaxis(t.reshape(b_, h_, n, c, t.shape[-1]), 2, 0)
    incl = jnp.tril(jnp.ones((c, c), dtype=bool))[..., None]

    def step(S0, xs):
        q_c, k_c, v_c, lf_c = xs
        Bc = jnp.cumsum(lf_c, axis=-2)
        dec = jnp.exp(jnp.where(incl, Bc[..., :, None, :] - Bc[..., None, :, :], -jnp.inf))
        A = jnp.sum(q_c[..., :, None, :] * k_c[..., None, :, :] * dec, axis=-1)
        o = (jnp.einsum("bhrk,bhkv->bhrv", q_c * jnp.exp(Bc), S0)
             + jnp.einsum("bhrj,bhjv->bhrv", A, v_c))
        S1 = (jnp.exp(Bc[..., -1, :])[..., None] * S0
              + jnp.einsum("bhjk,bhjv->bhkv", k_c * jnp.exp(Bc[..., -1:, :] - Bc), v_c))
        return S1, o

    S0 = jnp.zeros((b_, h_, dk, dv), q.dtype)
    _, o = lax.scan(step, S0, (to_chunks(q), to_chunks(k), to_chunks(v), to_chunks(logf)))
    return jnp.moveaxis(o, 0, 2).reshape(b_, h_, s_, dv)


def _bidir(scan_fn, shared, fwd, bwd):
    flip = lambda t: jnp.flip(t, axis=2)
    o_f = scan_fn(*shared, *fwd)
    o_b = scan_fn(*[flip(t) for t in shared], *[flip(t) for t in bwd])
    return o_f + flip(o_b)


def setup_inputs(seed: int = 0) -> dict:
    key = jax.random.key(seed)
    ks = jax.random.split(key, 16)
    f32 = jnp.float32
    nrm = lambda k, shape, fan_in: jax.random.normal(k, shape, f32) * fan_in ** -0.5
    gain = lambda k, shape: 1.0 + 0.02 * jax.random.normal(k, shape, f32)
    dt = jnp.exp(jax.random.uniform(ks[5], (DEPTH, 2, GDN_HEADS), f32,
                                    np.log(1e-3).astype(np.float32), np.log(0.1).astype(np.float32)))
    return {
        "x": jax.random.normal(ks[0], (BATCH, SEQ, D_MODEL), f32),
        "norm1_w": gain(ks[1], (DEPTH, D_MODEL)),
        "w_in": nrm(ks[2], (DEPTH, D_MODEL, D_IN), D_MODEL),
        "conv_w": nrm(ks[3], (DEPTH, CONV_K, GDN_CONV_CH), CONV_K),
        "gdn_a_log": jnp.log(jax.random.uniform(ks[4], (DEPTH, 2, GDN_HEADS), f32, 1.0, 16.0)),
        "gdn_dt_bias": dt + jnp.log(-jnp.expm1(-dt)),
        "gdn_norm_w": gain(ks[6], (DEPTH, GDN_DV)),
        "hgrn_lb_logits": 0.1 * jax.random.normal(ks[7], (DEPTH + 1, 2, HB_K), f32),
        "hgrn_norm_w": gain(ks[8], (DEPTH, HGRN_DV)),
        "w_out": nrm(ks[9], (DEPTH, D_MIX, D_MODEL), D_MIX),
        "norm2_w": gain(ks[10], (DEPTH, D_MODEL)),
        "w_gate": nrm(ks[11], (DEPTH, D_MODEL, D_FF), D_MODEL),
        "w_up": nrm(ks[12], (DEPTH, D_MODEL, D_FF), D_MODEL),
        "w_down": nrm(ks[13], (DEPTH, D_FF, D_MODEL), D_FF),
        "norm_f_w": gain(ks[14], (D_MODEL,)),
    }


def reference(x, norm1_w, w_in, conv_w, gdn_a_log, gdn_dt_bias, gdn_norm_w, hgrn_lb_logits,
              hgrn_norm_w, w_out, norm2_w, w_gate, w_up, w_down, norm_f_w):
    f32 = jnp.float32
    lb_all = jnp.cumsum(jax.nn.softmax(hgrn_lb_logits.astype(f32), axis=0), axis=0)
    for l in range(DEPTH):
        h = _rmsnorm(x, norm1_w[l])
        proj = h @ w_in[l]
        qkv_a, z_a, beta_a, alpha_a, q_b, f_b, i_b, g_b = _split_cols(proj)

        qkv_a = jax.nn.silu(_short_conv(qkv_a, conv_w[l])).astype(f32)
        q_a = _l2norm(_heads(qkv_a[..., :GA_QK], GDN_HEADS)) * GDN_DK ** -0.5
        k_a = _l2norm(_heads(qkv_a[..., GA_QK:2 * GA_QK], GDN_HEADS))
        v_a = _heads(qkv_a[..., 2 * GA_QK:], GDN_HEADS)
        beta = jax.nn.sigmoid(_dir_params(beta_a.astype(f32), GDN_HEADS))
        a_raw = _dir_params(alpha_a.astype(f32), GDN_HEADS)
        g = (-jnp.exp(gdn_a_log[l].astype(f32))[:, None, :, None]
             * jax.nn.softplus(a_raw + gdn_dt_bias[l].astype(f32)[:, None, :, None]))
        o_a = _bidir(_gdn_scan, (q_a, k_a, v_a), (g[0], beta[0]), (g[1], beta[1]))
        z_h = _heads(z_a.astype(f32), GDN_HEADS)
        y_a = _merge_heads(_rmsnorm(o_a, gdn_norm_w[l]) * jax.nn.silu(z_h))

        lb = lb_all[l]
        f_raw = f_b.astype(f32).reshape(f_b.shape[0], f_b.shape[1], 2, HB_K)
        f = lb + (1.0 - lb) * jax.nn.sigmoid(f_raw)
        f_fw, f_bw = _heads(f[:, :, 0], HGRN_HEADS), _heads(f[:, :, 1], HGRN_HEADS)
        qh_b = _heads(jax.nn.silu(q_b.astype(f32)), HGRN_HEADS)
        ih_b = _heads(i_b.astype(f32), HGRN_HEADS)
        o_b = _bidir(_hgrn2_scan, (qh_b, ih_b),
                     (1.0 - f_fw, jnp.log(f_fw)), (1.0 - f_bw, jnp.log(f_bw)))
        gh_b = _heads(g_b.astype(f32), HGRN_HEADS)
        y_b = _merge_heads(_rmsnorm(o_b, hgrn_norm_w[l]) * jax.nn.sigmoid(gh_b))

        y = jnp.concatenate([y_a, y_b], axis=-1).astype(x.dtype)
        x = x + y @ w_out[l]

        h2 = _rmsnorm(x, norm2_w[l])
        x = x + (jax.nn.silu(h2 @ w_gate[l]) * (h2 @ w_up[l])) @ w_down[l]
    return _rmsnorm(x, norm_f_w)
```

```python
import functools

import jax
import jax.numpy as jnp
from jax import lax
from jax.experimental import pallas as pl
from jax.experimental.pallas import tpu as pltpu

F32 = jnp.float32
BF16 = jnp.bfloat16

NORM_EPS = 1e-6
HEAD_DIM = 128
GDN_HEADS = 8
HGRN_HEADS = 8
CONV_K = 5
CONV_HALO = 8
SUBLANES = 8

SCAN_ROWS = 256
INPROJ_TM, INPROJ_TN = 1024, 512
CONV_TR, CONV_TC = 512, 512
OUTPROJ_TM = 256
FFN_TM, FFN_TF = 512, 512
VMEM_LIMIT = 56 * 1024 * 1024


def _params(sem):
    return pltpu.CompilerParams(dimension_semantics=sem, vmem_limit_bytes=VMEM_LIMIT)


def _dot(a, b):
    return jnp.dot(a.astype(BF16), b.astype(BF16), preferred_element_type=F32)


def _dot_nt(a, b):
    return lax.dot_general(a.astype(BF16), b.astype(BF16), (((1,), (1,)), ((), ())),
                           preferred_element_type=F32)


def _dot_tn(a, b):
    return lax.dot_general(a.astype(BF16), b.astype(BF16), (((0,), (0,)), ((), ())),
                           preferred_element_type=F32)


def _dot_01(m01, x):
    m = m01.astype(BF16)
    x1 = x.astype(BF16)
    r1 = x - x1.astype(F32)
    x2 = r1.astype(BF16)
    x3 = (r1 - x2.astype(F32)).astype(BF16)
    d = lambda p: jnp.dot(m, p, preferred_element_type=F32)
    return d(x1) + d(x2) + d(x3)


def _rms(x, w):
    return x * lax.rsqrt(jnp.mean(x * x, axis=-1, keepdims=True) + NORM_EPS) * w


def _inproj_kernel(x_ref, nw_ref, w_ref, wba_ref, o_ref, ba_ref, h_ref):
    @pl.when(pl.program_id(1) == 0)
    def _():
        hb = _rms(x_ref[...], nw_ref[...]).astype(BF16)
        h_ref[...] = hb
        ba_ref[...] = jnp.dot(hb, wba_ref[...], preferred_element_type=F32)

    o_ref[...] = jnp.dot(h_ref[...], w_ref[...], preferred_element_type=F32)


def _in_proj(x2, norm_w, w_main, w_ba):
    t, d = x2.shape
    n = w_main.shape[1]
    tm, tn = min(INPROJ_TM, t), INPROJ_TN
    return pl.pallas_call(
        _inproj_kernel,
        grid=(t // tm, n // tn),
        in_specs=[
            pl.BlockSpec((tm, d), lambda i, j: (i, 0)),
            pl.BlockSpec((1, d), lambda i, j: (0, 0)),
            pl.BlockSpec((d, tn), lambda i, j: (0, j)),
            pl.BlockSpec((d, HEAD_DIM), lambda i, j: (0, 0)),
        ],
        out_specs=[
            pl.BlockSpec((tm, tn), lambda i, j: (i, j)),
            pl.BlockSpec((tm, HEAD_DIM), lambda i, j: (i, 0)),
        ],
        out_shape=[jax.ShapeDtypeStruct((t, n), F32), jax.ShapeDtypeStruct((t, HEAD_DIM), F32)],
        scratch_shapes=[pltpu.VMEM((tm, d), BF16)],
        compiler_params=_params(("parallel", "arbitrary")),
        name="in_proj",
    )(x2, norm_w, w_main, w_ba)


def _conv_kernel(cur_ref, prev_ref, next_ref, w_ref, o_ref, ext_ref, *, blocks_per_seq, qk_blocks):
    i = pl.program_id(0)
    j = pl.program_id(1)
    tr = cur_ref.shape[0]
    pos = i % blocks_per_seq
    ext_ref[0:CONV_HALO, :] = jnp.where(pos == 0, 0.0, prev_ref[...])
    ext_ref[CONV_HALO:CONV_HALO + tr, :] = cur_ref[...]
    ext_ref[CONV_HALO + tr:, :] = jnp.where(pos == blocks_per_seq - 1, 0.0, next_ref[...])
    left = (CONV_K - 1) // 2
    acc = None
    for tap in range(CONV_K):
        start = CONV_HALO - left + tap
        term = w_ref[tap:tap + 1, :] * ext_ref[start:start + tr, :]
        acc = term if acc is None else acc + term
    act = acc * jax.nn.sigmoid(acc)
    is_q = j < qk_blocks
    is_qk = j < 2 * qk_blocks
    scale = jnp.where(is_q, HEAD_DIM ** -0.5, 1.0)
    for g in range(act.shape[1] // HEAD_DIM):
        seg = act[:, g * HEAD_DIM:(g + 1) * HEAD_DIM]
        normed = seg * lax.rsqrt(jnp.sum(seg * seg, axis=-1, keepdims=True) + NORM_EPS) * scale
        o_ref[:, g * HEAD_DIM:(g + 1) * HEAD_DIM] = jnp.where(is_qk, normed, seg)


def _conv_prep(proj, conv_w, seq):
    t = proj.shape[0]
    ch = conv_w.shape[1]
    tr, tc = min(CONV_TR, seq), CONV_TC
    halo_blocks = tr // CONV_HALO
    last_halo = t // CONV_HALO - 1
    kern = functools.partial(_conv_kernel, blocks_per_seq=seq // tr, qk_blocks=(ch // 3) // tc)
    return pl.pallas_call(
        kern,
        grid=(t // tr, ch // tc),
        in_specs=[
            pl.BlockSpec((tr, tc), lambda i, j: (i, j)),
            pl.BlockSpec((CONV_HALO, tc), lambda i, j: (jnp.maximum(i * halo_blocks - 1, 0), j)),
            pl.BlockSpec((CONV_HALO, tc), lambda i, j: (jnp.minimum((i + 1) * halo_blocks, last_halo), j)),
            pl.BlockSpec((CONV_K, tc), lambda i, j: (0, j)),
        ],
        out_specs=pl.BlockSpec((tr, tc), lambda i, j: (i, j)),
        out_shape=jax.ShapeDtypeStruct((t, ch), F32),
        scratch_shapes=[pltpu.VMEM((tr + 2 * CONV_HALO, tc), F32)],
        compiler_params=_params(("parallel", "parallel")),
        name="conv_prep",
    )(proj, proj, proj, conv_w)


def _unit_tri_inverse(l_mat, row, col):
    n = l_mat.shape[0]
    eye = (row == col).astype(F32)
    same = lambda s: (row // s) == (col // s)
    ld = jnp.where(same(SUBLANES), l_mat, 0.0)
    inv = eye - ld
    power = _dot(ld, ld)
    inv = inv + _dot(inv, power)
    power = _dot(power, power)
    inv = inv + _dot(inv, power)
    s = SUBLANES
    while s < n:
        off = jnp.where(same(2 * s) & jnp.logical_not(same(s)), l_mat, 0.0)
        inv = inv - _dot(inv, _dot(off, inv))
        s *= 2
    return inv


def _gdn_block(q, k, v, ba, a_log, dt_bias, head, direction, s_ref):
    n = q.shape[0]
    fwd = direction == 0
    lane = lax.broadcasted_iota(jnp.int32, ba.shape, 1)
    pick = lambda idx: jnp.sum(jnp.where(lane == idx, ba, 0.0), axis=1, keepdims=True)
    beta = jax.nn.sigmoid(pick(direction * GDN_HEADS + head))
    a_raw = pick(2 * GDN_HEADS + direction * GDN_HEADS + head)
    g = -jnp.exp(jnp.zeros_like(a_raw) + a_log) * jax.nn.softplus(a_raw + dt_bias)

    row = lax.broadcasted_iota(jnp.int32, (n, n), 0)
    col = lax.broadcasted_iota(jnp.int32, (n, n), 1)
    if fwd:
        earlier = col < row
        tri = col <= row
        first, last = 0, n - 1
    else:
        earlier = col > row
        tri = col >= row
        first, last = n - 1, 0
    after = (row > col) if fwd else (row < col)
    ld = _dot_01(tri, jnp.where(after, g, 0.0))
    gcum = ld[:, first:first + 1] + g[first:first + 1, :]
    g_end = gcum[last:last + 1, :]
    decay = jnp.exp(jnp.where(earlier | (row == col), ld, -jnp.inf))
    gamma = jnp.exp(gcum)

    kk = _dot_nt(k, k)
    qk = _dot_nt(q, k)
    l_mat = beta * kk * jnp.where(earlier, decay, 0.0)
    a_qk = qk * decay
    t_inv = _unit_tri_inverse(l_mat, row, col)
    rhs = jnp.concatenate([beta * v, (beta * gamma) * k], axis=1)
    sol = _dot(t_inv, rhs)
    u, wk = sol[:, :HEAD_DIM], sol[:, HEAD_DIM:]

    s0 = s_ref[...]
    w = u - _dot(wk, s0)
    o = _dot(q * gamma, s0) + _dot(a_qk, w)
    s_ref[...] = jnp.exp(g_end) * s0 + _dot_tn(k * jnp.exp(g_end - gcum), w)
    return o


def _gdn_kernel(alog_ref, dtb_ref, qf_ref, kf_ref, vf_ref, baf_ref, qb_ref, kb_ref, vb_ref, bab_ref,
                of_ref, ob_ref, sf_ref, sb_ref):
    h = pl.program_id(1)

    @pl.when(pl.program_id(2) == 0)
    def _():
        sf_ref[...] = jnp.zeros_like(sf_ref)
        sb_ref[...] = jnp.zeros_like(sb_ref)

    of_ref[...] = _gdn_block(qf_ref[...], kf_ref[...], vf_ref[...], baf_ref[...],
                             alog_ref[0, h], dtb_ref[0, h], h, 0, sf_ref)
    ob_ref[...] = _gdn_block(qb_ref[...], kb_ref[...], vb_ref[...], bab_ref[...],
                             alog_ref[1, h], dtb_ref[1, h], h, 1, sb_ref)


def _gdn_scan(qkv, ba, a_log, dt_bias, batch, seq):
    t = qkv.shape[0]
    r = min(SCAN_ROWS, seq)
    nb = seq // r
    fwd_rows = lambda b, n: b * nb + n
    bwd_rows = lambda b, n: b * nb + (nb - 1 - n)

    def spec(rows, col_base):
        return pl.BlockSpec((r, HEAD_DIM), lambda b, h, n: (rows(b, n), col_base + h))

    ba_spec = lambda rows: pl.BlockSpec((r, HEAD_DIM), lambda b, h, n: (rows(b, n), 0))
    smem = pl.BlockSpec(memory_space=pltpu.SMEM)
    out_sds = jax.ShapeDtypeStruct((t, GDN_HEADS * HEAD_DIM), F32)
    return pl.pallas_call(
        _gdn_kernel,
        grid=(batch, GDN_HEADS, nb),
        in_specs=[smem, smem,
                  spec(fwd_rows, 0), spec(fwd_rows, GDN_HEADS), spec(fwd_rows, 2 * GDN_HEADS), ba_spec(fwd_rows),
                  spec(bwd_rows, 0), spec(bwd_rows, GDN_HEADS), spec(bwd_rows, 2 * GDN_HEADS), ba_spec(bwd_rows)],
        out_specs=[spec(fwd_rows, 0), spec(bwd_rows, 0)],
        out_shape=[out_sds, out_sds],
        scratch_shapes=[pltpu.VMEM((HEAD_DIM, HEAD_DIM), F32), pltpu.VMEM((HEAD_DIM, HEAD_DIM), F32)],
        compiler_params=_params(("parallel", "parallel", "arbitrary")),
        name="gdn_scan",
    )(a_log, dt_bias, qkv, qkv, qkv, ba, qkv, qkv, qkv, ba)


def _hgrn_block(q_raw, f_raw, v, lb, direction, st_ref):
    n = q_raw.shape[0]
    fwd = direction == 0
    q = q_raw * jax.nn.sigmoid(q_raw)
    f = lb + (1.0 - lb) * jax.nn.sigmoid(f_raw)
    k = 1.0 - f
    logf = jnp.log(f)

    row = lax.broadcasted_iota(jnp.int32, (n, n), 0)
    col = lax.broadcasted_iota(jnp.int32, (n, n), 1)
    tri = (col <= row) if fwd else (col >= row)
    bc = _dot_01(tri, logf)

    same_tile = (row // SUBLANES) == (col // SUBLANES)
    a = jnp.zeros((n, n), F32)
    for delta in range(SUBLANES):
        shift = delta if fwd else (n - delta) % n
        k_s = k if delta == 0 else pltpu.roll(k, shift, 0)
        b_s = bc if delta == 0 else pltpu.roll(bc, shift, 0)
        pair = jnp.sum(q * k_s * jnp.exp(bc - b_s), axis=1, keepdims=True)
        src = (row - delta) if fwd else (row + delta)
        a = a + jnp.where(same_tile & (col == src), pair, 0.0)

    rowv = lax.broadcasted_iota(jnp.int32, (n, HEAD_DIM), 0)
    s = SUBLANES
    while s < n:
        groups = n // (2 * s)
        idx = s - 1 if fwd else s
        b3 = bc.reshape(groups, 2 * s, HEAD_DIM)
        bref = jnp.broadcast_to(b3[:, idx:idx + 1, :], b3.shape).reshape(n, HEAD_DIM)
        upper_half = (rowv % (2 * s)) >= s
        q_side = upper_half if fwd else jnp.logical_not(upper_half)
        q_l = q * jnp.exp(jnp.where(q_side, bc - bref, -jnp.inf))
        k_l = k * jnp.exp(jnp.where(q_side, -jnp.inf, bref - bc))
        a = a + jnp.where((row // (2 * s)) == (col // (2 * s)), _dot_nt(q_l, k_l), 0.0)
        s *= 2

    last = n - 1 if fwd else 0
    b_end = bc[last:last + 1, :]
    st0 = st_ref[...]
    o = _dot_nt(q * jnp.exp(bc), st0) + _dot(a, v)
    st_ref[...] = st0 * jnp.exp(b_end) + _dot_tn(v, k * jnp.exp(b_end - bc))
    return o


def _hgrn_kernel(lg_ref, qf_ref, ff_ref, vf_ref, qb_ref, fb_ref, vb_ref, of_ref, ob_ref, sf_ref, sb_ref):
    @pl.when(pl.program_id(2) == 0)
    def _():
        sf_ref[...] = jnp.zeros_like(sf_ref)
        sb_ref[...] = jnp.zeros_like(sb_ref)

    lg = lg_ref[...]
    e = jnp.exp(lg - jnp.max(lg, axis=0, keepdims=True))
    lb = e[0] / jnp.sum(e, axis=0)
    of_ref[...] = _hgrn_block(qf_ref[...], ff_ref[...], vf_ref[...], lb[0:1, :], 0, sf_ref)
    ob_ref[...] = _hgrn_block(qb_ref[...], fb_ref[...], vb_ref[...], lb[1:2, :], 1, sb_ref)


def _hgrn_scan(proj, lb_logits, col_q, col_ffw, col_fbw, col_i, batch, seq):
    t = proj.shape[0]
    r = min(SCAN_ROWS, seq)
    nb = seq // r
    fwd_rows = lambda b, n: b * nb + n
    bwd_rows = lambda b, n: b * nb + (nb - 1 - n)

    def spec(rows, col_base):
        return pl.BlockSpec((r, HEAD_DIM), lambda b, h, n: (rows(b, n), col_base + h))

    layers = lb_logits.shape[0]
    out_sds = jax.ShapeDtypeStruct((t, HGRN_HEADS * HEAD_DIM), F32)
    return pl.pallas_call(
        _hgrn_kernel,
        grid=(batch, HGRN_HEADS, nb),
        in_specs=[pl.BlockSpec((layers, 2, HEAD_DIM), lambda b, h, n: (0, 0, h)),
                  spec(fwd_rows, col_q), spec(fwd_rows, col_ffw), spec(fwd_rows, col_i),
                  spec(bwd_rows, col_q), spec(bwd_rows, col_fbw), spec(bwd_rows, col_i)],
        out_specs=[spec(fwd_rows, 0), spec(bwd_rows, 0)],
        out_shape=[out_sds, out_sds],
        scratch_shapes=[pltpu.VMEM((HEAD_DIM, HEAD_DIM), F32), pltpu.VMEM((HEAD_DIM, HEAD_DIM), F32)],
        compiler_params=_params(("parallel", "parallel", "arbitrary")),
        name="hgrn_scan",
    )(lb_logits, proj, proj, proj, proj, proj, proj)


def _outproj_kernel(oaf_ref, oab_ref, obf_ref, obb_ref, z_ref, gb_ref, x_ref, wa_ref, wb_ref, wo_ref,
                    o_ref, y_ref):
    width = oaf_ref.shape[1]
    for h in range(width // HEAD_DIM):
        sl = slice(h * HEAD_DIM, (h + 1) * HEAD_DIM)
        z = z_ref[:, sl]
        ya = _rms(oaf_ref[:, sl] + oab_ref[:, sl], wa_ref[...]) * (z * jax.nn.sigmoid(z))
        y_ref[:, sl] = ya.astype(BF16)
        yb = _rms(obf_ref[:, sl] + obb_ref[:, sl], wb_ref[...]) * jax.nn.sigmoid(gb_ref[:, sl])
        y_ref[:, width + h * HEAD_DIM:width + (h + 1) * HEAD_DIM] = yb.astype(BF16)
    o_ref[...] = x_ref[...] + jnp.dot(y_ref[...], wo_ref[...], preferred_element_type=F32)


def _out_proj(oaf, oab, obf, obb, proj, col_z, col_g, x2, gdn_w, hgrn_w, w_out):
    t, d = x2.shape
    wa = oaf.shape[1]
    tm = min(OUTPROJ_TM, t)
    row_blk = lambda c: pl.BlockSpec((tm, wa), lambda i: (i, c))
    full = lambda shape: pl.BlockSpec(shape, lambda i: (0, 0))
    return pl.pallas_call(
        _outproj_kernel,
        grid=(t // tm,),
        in_specs=[row_blk(0), row_blk(0), row_blk(0), row_blk(0),
                  row_blk(col_z // wa), row_blk(col_g // wa),
                  pl.BlockSpec((tm, d), lambda i: (i, 0)),
                  full((1, HEAD_DIM)), full((1, HEAD_DIM)), full(w_out.shape)],
        out_specs=pl.BlockSpec((tm, d), lambda i: (i, 0)),
        out_shape=jax.ShapeDtypeStruct((t, d), F32),
        scratch_shapes=[pltpu.VMEM((tm, w_out.shape[0]), BF16)],
        compiler_params=_params(("parallel",)),
        name="out_proj",
    )(oaf, oab, obf, obb, proj, proj, x2, gdn_w, hgrn_w, w_out)


def _ffn_kernel(x_ref, n2_ref, nf_ref, wg_ref, wu_ref, wd_ref, o_ref, h_ref, acc_ref):
    j = pl.program_id(1)

    @pl.when(j == 0)
    def _():
        h_ref[...] = _rms(x_ref[...], n2_ref[...]).astype(BF16)
        acc_ref[...] = jnp.zeros_like(acc_ref)

    h = h_ref[...]
    gate = jnp.dot(h, wg_ref[...], preferred_element_type=F32)
    up = jnp.dot(h, wu_ref[...], preferred_element_type=F32)
    act = (gate * jax.nn.sigmoid(gate) * up).astype(BF16)
    acc_ref[...] += jnp.dot(act, wd_ref[...], preferred_element_type=F32)

    @pl.when(j == pl.num_programs(1) - 1)
    def _():
        o_ref[...] = _rms(x_ref[...] + acc_ref[...], nf_ref[...])


def _ffn(x1, norm2_w, normf_w, w_gate, w_up, w_down):
    t, d = x1.shape
    f = w_gate.shape[1]
    tm, tf = min(FFN_TM, t), FFN_TF
    return pl.pallas_call(
        _ffn_kernel,
        grid=(t // tm, f // tf),
        in_specs=[
            pl.BlockSpec((tm, d), lambda i, j: (i, 0)),
            pl.BlockSpec((1, d), lambda i, j: (0, 0)),
            pl.BlockSpec((1, d), lambda i, j: (0, 0)),
            pl.BlockSpec((d, tf), lambda i, j: (0, j)),
            pl.BlockSpec((d, tf), lambda i, j: (0, j)),
            pl.BlockSpec((tf, d), lambda i, j: (j, 0)),
        ],
        out_specs=pl.BlockSpec((tm, d), lambda i, j: (i, 0)),
        out_shape=jax.ShapeDtypeStruct((t, d), F32),
        scratch_shapes=[pltpu.VMEM((tm, d), BF16), pltpu.VMEM((tm, d), F32)],
        compiler_params=_params(("parallel", "arbitrary")),
        name="ffn",
    )(x1, norm2_w, normf_w, w_gate, w_up, w_down)


def kernel(x, norm1_w, w_in, conv_w, gdn_a_log, gdn_dt_bias, gdn_norm_w, hgrn_lb_logits, hgrn_norm_w,
           w_out, norm2_w, w_gate, w_up, w_down, norm_f_w):
    batch, seq, d = x.shape
    depth = w_in.shape[0]
    assert depth == 1 and seq % min(SCAN_ROWS, seq) == 0
    ga = GDN_HEADS * HEAD_DIM
    hb = HGRN_HEADS * HEAD_DIM
    n_gate = 4 * GDN_HEADS
    gate_lo = 4 * ga
    x2 = x.reshape(batch * seq, d)

    w = w_in[0]
    w_main = jnp.concatenate([w[:, :gate_lo], w[:, gate_lo + n_gate:]], axis=1).astype(BF16)
    w_ba = jnp.pad(w[:, gate_lo:gate_lo + n_gate], ((0, 0), (0, HEAD_DIM - n_gate))).astype(BF16)
    col_z = 3 * ga
    col_qb = 4 * ga
    col_ffw = col_qb + hb
    col_fbw = col_ffw + hb
    col_i = col_fbw + hb
    col_g = col_i + hb

    proj, ba = _in_proj(x2, norm1_w, w_main, w_ba)
    qkv = _conv_prep(proj, conv_w[0], seq)
    oaf, oab = _gdn_scan(qkv, ba, gdn_a_log[0], gdn_dt_bias[0], batch, seq)
    obf, obb = _hgrn_scan(proj, hgrn_lb_logits, col_qb // HEAD_DIM, col_ffw // HEAD_DIM,
                          col_fbw // HEAD_DIM, col_i // HEAD_DIM, batch, seq)
    x1 = _out_proj(oaf, oab, obf, obb, proj, col_z, col_g, x2, gdn_norm_w, hgrn_norm_w,
                   w_out[0].astype(BF16))
    out = _ffn(x1, norm2_w, norm_f_w.reshape(1, d), w_gate[0].astype(BF16), w_up[0].astype(BF16),
               w_down[0].astype(BF16))
    return out.reshape(batch, seq, d)
```

```python
import functools

import jax
import jax.numpy as jnp
from jax import lax
from jax.experimental import pallas as pl
from jax.experimental.pallas import tpu as pltpu

F32 = jnp.float32
BF16 = jnp.bfloat16

NORM_EPS = 1e-6
HEAD_DIM = 128
GDN_HEADS = 8
HGRN_HEADS = 8
CONV_K = 5
CONV_HALO = 8
SUBLANES = 8

SCAN_ROWS = 256
SCAN_HEADS = 2
INPROJ_TM, INPROJ_TN = 1024, 512
CONV_TR, CONV_TC = 512, 512
OUTPROJ_TM = 256
FFN_TM, FFN_TF = 512, 512
VMEM_LIMIT = 56 * 1024 * 1024


def _params(sem):
    return pltpu.CompilerParams(dimension_semantics=sem, vmem_limit_bytes=VMEM_LIMIT)


def _bf(a):
    return a.astype(BF16)


def _mm(a, b):
    return jnp.dot(a, b, preferred_element_type=F32)


def _mm_nt(a, b):
    return lax.dot_general(a, b, (((1,), (1,)), ((), ())), preferred_element_type=F32)


def _mm_tn(a, b):
    return lax.dot_general(a, b, (((0,), (0,)), ((), ())), preferred_element_type=F32)


def _dot_01(m01_bf, x):
    hi = _bf(x)
    lo = _bf(x - hi.astype(F32))
    return _mm(m01_bf, hi) + _mm(m01_bf, lo)


def _rms(x, w):
    return x * lax.rsqrt(jnp.mean(x * x, axis=-1, keepdims=True) + NORM_EPS) * w


def _inproj_kernel(x_ref, nw_ref, w_ref, wba_ref, o_ref, ba_ref, h_ref):
    @pl.when(pl.program_id(1) == 0)
    def _():
        hb = _bf(_rms(x_ref[...], nw_ref[...]))
        h_ref[...] = hb
        ba_ref[...] = _mm(hb, wba_ref[...])

    o_ref[...] = _mm(h_ref[...], w_ref[...])


def _in_proj(x2, norm_w, w_main, w_ba):
    t, d = x2.shape
    n = w_main.shape[1]
    tm, tn = min(INPROJ_TM, t), INPROJ_TN
    return pl.pallas_call(
        _inproj_kernel,
        grid=(t // tm, n // tn),
        in_specs=[
            pl.BlockSpec((tm, d), lambda i, j: (i, 0)),
            pl.BlockSpec((1, d), lambda i, j: (0, 0)),
            pl.BlockSpec((d, tn), lambda i, j: (0, j)),
            pl.BlockSpec((d, HEAD_DIM), lambda i, j: (0, 0)),
        ],
        out_specs=[
            pl.BlockSpec((tm, tn), lambda i, j: (i, j)),
            pl.BlockSpec((tm, HEAD_DIM), lambda i, j: (i, 0)),
        ],
        out_shape=[jax.ShapeDtypeStruct((t, n), F32), jax.ShapeDtypeStruct((t, HEAD_DIM), F32)],
        scratch_shapes=[pltpu.VMEM((tm, d), BF16)],
        compiler_params=_params(("parallel", "arbitrary")),
        name="in_proj",
    )(x2, norm_w, w_main, w_ba)


def _conv_kernel(cur_ref, prev_ref, next_ref, w_ref, o_ref, ext_ref, *, blocks_per_seq, qk_blocks):
    i = pl.program_id(0)
    j = pl.program_id(1)
    tr = cur_ref.shape[0]
    pos = i % blocks_per_seq
    ext_ref[0:CONV_HALO, :] = jnp.where(pos == 0, 0.0, prev_ref[...])
    ext_ref[CONV_HALO:CONV_HALO + tr, :] = cur_ref[...]
    ext_ref[CONV_HALO + tr:, :] = jnp.where(pos == blocks_per_seq - 1, 0.0, next_ref[...])
    left = (CONV_K - 1) // 2
    acc = None
    for tap in range(CONV_K):
        start = CONV_HALO - left + tap
        term = w_ref[tap:tap + 1, :] * ext_ref[start:start + tr, :]
        acc = term if acc is None else acc + term
    act = acc * jax.nn.sigmoid(acc)
    is_q = j < qk_blocks
    is_qk = j < 2 * qk_blocks
    scale = jnp.where(is_q, HEAD_DIM ** -0.5, 1.0)
    for g in range(act.shape[1] // HEAD_DIM):
        seg = act[:, g * HEAD_DIM:(g + 1) * HEAD_DIM]
        normed = seg * lax.rsqrt(jnp.sum(seg * seg, axis=-1, keepdims=True) + NORM_EPS) * scale
        o_ref[:, g * HEAD_DIM:(g + 1) * HEAD_DIM] = jnp.where(is_qk, normed, seg)


def _conv_prep(proj, conv_w, seq):
    t = proj.shape[0]
    ch = conv_w.shape[1]
    tr, tc = min(CONV_TR, seq), CONV_TC
    halo_blocks = tr // CONV_HALO
    last_halo = t // CONV_HALO - 1
    kern = functools.partial(_conv_kernel, blocks_per_seq=seq // tr, qk_blocks=(ch // 3) // tc)
    return pl.pallas_call(
        kern,
        grid=(t // tr, ch // tc),
        in_specs=[
            pl.BlockSpec((tr, tc), lambda i, j: (i, j)),
            pl.BlockSpec((CONV_HALO, tc), lambda i, j: (jnp.maximum(i * halo_blocks - 1, 0), j)),
            pl.BlockSpec((CONV_HALO, tc), lambda i, j: (jnp.minimum((i + 1) * halo_blocks, last_halo), j)),
            pl.BlockSpec((CONV_K, tc), lambda i, j: (0, j)),
        ],
        out_specs=pl.BlockSpec((tr, tc), lambda i, j: (i, j)),
        out_shape=jax.ShapeDtypeStruct((t, ch), F32),
        scratch_shapes=[pltpu.VMEM((tr + 2 * CONV_HALO, tc), F32)],
        compiler_params=_params(("parallel", "parallel")),
        name="conv_prep",
    )(proj, proj, proj, conv_w)


def _scan_specs(seq):
    r = min(SCAN_ROWS, seq)
    nb = seq // r
    width = SCAN_HEADS * HEAD_DIM
    fwd_rows = lambda b, n: b * nb + n
    bwd_rows = lambda b, n: b * nb + (nb - 1 - n)

    def spec(rows, col_base):
        return pl.BlockSpec((r, width), lambda b, h, n: (rows(b, n), col_base // width + h))

    return r, nb, fwd_rows, bwd_rows, spec


def _head(ref, hh):
    return ref[:, hh * HEAD_DIM:(hh + 1) * HEAD_DIM]


def _unit_tri_inverses(l_mats, xor_rc):
    n = l_mats[0].shape[0]
    eye = jnp.where(xor_rc == 0, 1.0, 0.0)
    in_tile = xor_rc < SUBLANES
    lds = [jnp.where(in_tile, l, 0.0) for l in l_mats]
    invs = [eye - ld for ld in lds]
    powers = [_bf(ld) for ld in lds]
    for _ in range(2):
        powers = [_mm(p, p) for p in powers]
        powers_bf = [_bf(p) for p in powers]
        invs = [inv + _mm(_bf(inv), p) for inv, p in zip(invs, powers_bf)]
        powers = powers_bf
    s = SUBLANES
    while s < n:
        level = (xor_rc // s) == 1
        offs = [_bf(jnp.where(level, l, 0.0)) for l in l_mats]
        invs_bf = [_bf(inv) for inv in invs]
        xs = [_bf(_mm(off, inv)) for off, inv in zip(offs, invs_bf)]
        invs = [inv - _mm(inv_bf, x) for inv, inv_bf, x in zip(invs, invs_bf, xs)]
        s *= 2
    return invs


def _gdn_chains(chains, n):
    row = lax.broadcasted_iota(jnp.int32, (n, n), 0)
    col = lax.broadcasted_iota(jnp.int32, (n, n), 1)
    xor_rc = row ^ col
    earlier = {True: col < row, False: col > row}
    incl_bf = {d: _bf(jnp.where(earlier[d] | (row == col), 1.0, 0.0)) for d in (True, False)}
    ends = {True: (0, n - 1), False: (n - 1, 0)}

    lds = [_dot_01(incl_bf[c["fwd"]], jnp.where(earlier[c["fwd"]], c["g"], 0.0)) for c in chains]
    gcums, g_ends, decays, gammas = [], [], [], []
    for c, ld in zip(chains, lds):
        first, last = ends[c["fwd"]]
        gcum = ld[:, first:first + 1] + c["g"][first:first + 1, :]
        gcums.append(gcum)
        g_ends.append(gcum[last:last + 1, :])
        decays.append(jnp.exp(jnp.where(earlier[c["fwd"]] | (row == col), ld, -jnp.inf)))
        gammas.append(jnp.exp(gcum))

    k_bf = [_bf(c["k"]) for c in chains]
    qk_kk = [_mm_nt(_bf(jnp.concatenate([c["q"], c["k"]], axis=0)), kb) for c, kb in zip(chains, k_bf)]
    a_qk = [_bf(m[:n] * dec) for m, dec in zip(qk_kk, decays)]
    l_mats = [c["beta"] * m[n:] * jnp.where(earlier[c["fwd"]], dec, 0.0)
              for c, m, dec in zip(chains, qk_kk, decays)]
    t_invs = _unit_tri_inverses(l_mats, xor_rc)
    rhs = [_bf(jnp.concatenate([c["beta"] * c["v"], (c["beta"] * gam) * c["k"]], axis=1))
           for c, gam in zip(chains, gammas)]
    sols = [_mm(_bf(t), r) for t, r in zip(t_invs, rhs)]

    s0 = [c["s_ref"][...] for c in chains]
    s0_bf = [_bf(s) for s in s0]
    lhs = [_bf(jnp.concatenate([sol[:, HEAD_DIM:], c["q"] * gam], axis=0))
           for c, sol, gam in zip(chains, sols, gammas)]
    ws = [_mm(l, s) for l, s in zip(lhs, s0_bf)]
    w_bf = [_bf(sol[:, :HEAD_DIM] - x[:n]) for sol, x in zip(sols, ws)]
    outs = [x[n:] + _mm(a, w) for x, a, w in zip(ws, a_qk, w_bf)]
    for c, s, w, ge, gc in zip(chains, s0, w_bf, g_ends, gcums):
        k_dec = _bf(c["k"] * jnp.exp(ge - gc))
        c["s_ref"][...] = jnp.exp(ge) * s + _mm_tn(k_dec, w)
    return outs


def _gdn_kernel(alog_ref, dtb_ref, qf_ref, kf_ref, vf_ref, baf_ref, qb_ref, kb_ref, vb_ref, bab_ref,
                of_ref, ob_ref, s_ref):
    hg = pl.program_id(1)
    n = qf_ref.shape[0]

    @pl.when(pl.program_id(2) == 0)
    def _():
        s_ref[...] = jnp.zeros_like(s_ref)

    lane = lax.broadcasted_iota(jnp.int32, baf_ref.shape, 1)
    chains = []
    for direction, (q_ref, k_ref, v_ref, ba_ref) in enumerate(
            ((qf_ref, kf_ref, vf_ref, baf_ref), (qb_ref, kb_ref, vb_ref, bab_ref))):
        ba = ba_ref[...]
        for hh in range(SCAN_HEADS):
            head = hg * SCAN_HEADS + hh
            pick = lambda idx: jnp.sum(jnp.where(lane == idx, ba, 0.0), axis=1, keepdims=True)
            beta = jax.nn.sigmoid(pick(direction * GDN_HEADS + head))
            a_raw = pick((2 + direction) * GDN_HEADS + head)
            g = (-jnp.exp(jnp.zeros_like(a_raw) + alog_ref[direction, head])
                 * jax.nn.softplus(a_raw + dtb_ref[direction, head]))
            chains.append(dict(q=_head(q_ref, hh), k=_head(k_ref, hh), v=_head(v_ref, hh), beta=beta, g=g,
                               fwd=direction == 0, s_ref=s_ref.at[direction * SCAN_HEADS + hh]))
    outs = _gdn_chains(chains, n)
    for hh in range(SCAN_HEADS):
        of_ref[:, hh * HEAD_DIM:(hh + 1) * HEAD_DIM] = outs[hh]
        ob_ref[:, hh * HEAD_DIM:(hh + 1) * HEAD_DIM] = outs[SCAN_HEADS + hh]


def _gdn_scan(qkv, ba, a_log, dt_bias, batch, seq):
    t = qkv.shape[0]
    r, nb, fwd_rows, bwd_rows, spec = _scan_specs(seq)
    ga = GDN_HEADS * HEAD_DIM
    ba_spec = lambda rows: pl.BlockSpec((r, HEAD_DIM), lambda b, h, n: (rows(b, n), 0))
    smem = pl.BlockSpec(memory_space=pltpu.SMEM)
    out_sds = jax.ShapeDtypeStruct((t, ga), F32)
    return pl.pallas_call(
        _gdn_kernel,
        grid=(batch, GDN_HEADS // SCAN_HEADS, nb),
        in_specs=[smem, smem,
                  spec(fwd_rows, 0), spec(fwd_rows, ga), spec(fwd_rows, 2 * ga), ba_spec(fwd_rows),
                  spec(bwd_rows, 0), spec(bwd_rows, ga), spec(bwd_rows, 2 * ga), ba_spec(bwd_rows)],
        out_specs=[spec(fwd_rows, 0), spec(bwd_rows, 0)],
        out_shape=[out_sds, out_sds],
        scratch_shapes=[pltpu.VMEM((2 * SCAN_HEADS, HEAD_DIM, HEAD_DIM), F32)],
        compiler_params=_params(("parallel", "parallel", "arbitrary")),
        name="gdn_scan",
    )(a_log, dt_bias, qkv, qkv, qkv, ba, qkv, qkv, qkv, ba)


def _block_ref_rows(bc, s, idx):
    n, d = bc.shape
    span = 2 * s
    if span >= SUBLANES:
        b3 = bc.reshape(n // span, span, d)
        return jnp.broadcast_to(b3[:, idx:idx + 1, :], b3.shape).reshape(n, d)
    b3 = bc.reshape(n // SUBLANES, SUBLANES, d)
    sub = lax.broadcasted_iota(jnp.int32, b3.shape, 1)
    out = None
    for g in reversed(range(SUBLANES // span)):
        piece = jnp.broadcast_to(b3[:, g * span + idx:g * span + idx + 1, :], b3.shape)
        out = piece if out is None else jnp.where(sub < (g + 1) * span, piece, out)
    return out.reshape(n, d)


def _hgrn_chains(chains, n):
    half = n // 2
    row = lax.broadcasted_iota(jnp.int32, (n, n), 0)
    col = lax.broadcasted_iota(jnp.int32, (n, n), 1)
    incl_bf = {True: _bf(jnp.where(col <= row, 1.0, 0.0)), False: _bf(jnp.where(col >= row, 1.0, 0.0))}
    rowv = lax.broadcasted_iota(jnp.int32, (n, HEAD_DIM), 0)
    hrow = lax.broadcasted_iota(jnp.int32, (half, half), 0)
    hcol = lax.broadcasted_iota(jnp.int32, (half, half), 1)
    hxor = hrow ^ hcol
    keep = {True: hcol <= hrow, False: hcol >= hrow}

    bcs = [None] * len(chains)
    for d in (True, False):
        ids = [i for i, c in enumerate(chains) if c["fwd"] == d]
        if ids:
            cat = _dot_01(incl_bf[d], jnp.concatenate([chains[i]["lf"] for i in ids], axis=1))
            for pos, i in enumerate(ids):
                bcs[i] = cat[:, pos * HEAD_DIM:(pos + 1) * HEAD_DIM]

    diag_q = [[None, None] for _ in chains]
    cross = [None] * len(chains)
    s = 1
    while s < n:
        ms = []
        for c, bc in zip(chains, bcs):
            upper = (rowv % (2 * s)) >= s
            q_side = upper if c["fwd"] else jnp.logical_not(upper)
            bref = _block_ref_rows(bc, s, s - 1 if c["fwd"] else s)
            delta = bc - bref
            e = jnp.exp(jnp.where(q_side, delta, -delta))
            ms.append(_bf(jnp.where(q_side, c["q"], c["k"]) * e))
        if 2 * s < n:
            level = (hxor // s) == 1
            for i, m in enumerate(ms):
                for quad in range(2):
                    mq = m[quad * half:(quad + 1) * half]
                    prod = _mm_nt(mq, mq)
                    prev = diag_q[i][quad]
                    diag_q[i][quad] = jnp.where(level, prod, 0.0 if prev is None else prev)
        else:
            for i, (c, m) in enumerate(zip(chains, ms)):
                cross[i] = _mm_nt(m[half:], m[:half]) if c["fwd"] else _mm_nt(m[:half], m[half:])
        s *= 2

    outs = []
    for i, (c, bc) in enumerate(zip(chains, bcs)):
        fwd = c["fwd"]
        qk_diag = jnp.sum(c["q"] * c["k"], axis=1, keepdims=True)
        a_quads = []
        for quad in range(2):
            dq = qk_diag[quad * half:(quad + 1) * half]
            a_full = jnp.where(hxor == 0, dq, diag_q[i][quad])
            a_quads.append(_bf(jnp.where(keep[fwd], a_full, 0.0)))
        v_bf = _bf(c["v"])
        v_lo, v_hi = v_bf[:half], v_bf[half:]
        st0 = c["st_ref"][...]
        inter = _mm_nt(_bf(c["q"] * jnp.exp(bc)), _bf(st0))
        o_lo = _mm(a_quads[0], v_lo)
        o_hi = _mm(a_quads[1], v_hi)
        if fwd:
            o_hi = o_hi + _mm(_bf(cross[i]), v_lo)
        else:
            o_lo = o_lo + _mm(_bf(cross[i]), v_hi)
        outs.append(inter + jnp.concatenate([o_lo, o_hi], axis=0))
        last = n - 1 if fwd else 0
        b_end = bc[last:last + 1, :]
        c["st_ref"][...] = st0 * jnp.exp(b_end) + _mm_tn(v_bf, _bf(c["k"] * jnp.exp(b_end - bc)))
    return outs


def _hgrn_kernel(lg_ref, qf_ref, ff_ref, vf_ref, qb_ref, fb_ref, vb_ref, of_ref, ob_ref, st_ref):
    n = qf_ref.shape[0]

    @pl.when(pl.program_id(2) == 0)
    def _():
        st_ref[...] = jnp.zeros_like(st_ref)

    lg = lg_ref[...]
    e = jnp.exp(lg - jnp.max(lg, axis=0, keepdims=True))
    lb_all = e[0] / jnp.sum(e, axis=0)
    chains = []
    for direction, (q_ref, f_ref, v_ref) in enumerate(((qf_ref, ff_ref, vf_ref), (qb_ref, fb_ref, vb_ref))):
        for hh in range(SCAN_HEADS):
            lb = lb_all[direction:direction + 1, hh * HEAD_DIM:(hh + 1) * HEAD_DIM]
            q_raw = _head(q_ref, hh)
            f = lb + (1.0 - lb) * jax.nn.sigmoid(_head(f_ref, hh))
            chains.append(dict(q=q_raw * jax.nn.sigmoid(q_raw), k=1.0 - f, v=_head(v_ref, hh), lf=jnp.log(f),
                               fwd=direction == 0, st_ref=st_ref.at[direction * SCAN_HEADS + hh]))
    outs = _hgrn_chains(chains, n)
    for hh in range(SCAN_HEADS):
        of_ref[:, hh * HEAD_DIM:(hh + 1) * HEAD_DIM] = outs[hh]
        ob_ref[:, hh * HEAD_DIM:(hh + 1) * HEAD_DIM] = outs[SCAN_HEADS + hh]


def _hgrn_scan(proj, lb_logits, col_q, col_ffw, col_fbw, col_i, batch, seq):
    t = proj.shape[0]
    r, nb, fwd_rows, bwd_rows, spec = _scan_specs(seq)
    layers = lb_logits.shape[0]
    width = SCAN_HEADS * HEAD_DIM
    out_sds = jax.ShapeDtypeStruct((t, HGRN_HEADS * HEAD_DIM), F32)
    return pl.pallas_call(
        _hgrn_kernel,
        grid=(batch, HGRN_HEADS // SCAN_HEADS, nb),
        in_specs=[pl.BlockSpec((layers, 2, width), lambda b, h, n: (0, 0, h)),
                  spec(fwd_rows, col_q), spec(fwd_rows, col_ffw), spec(fwd_rows, col_i),
                  spec(bwd_rows, col_q), spec(bwd_rows, col_fbw), spec(bwd_rows, col_i)],
        out_specs=[spec(fwd_rows, 0), spec(bwd_rows, 0)],
        out_shape=[out_sds, out_sds],
        scratch_shapes=[pltpu.VMEM((2 * SCAN_HEADS, HEAD_DIM, HEAD_DIM), F32)],
        compiler_params=_params(("parallel", "parallel", "arbitrary")),
        name="hgrn_scan",
    )(lb_logits, proj, proj, proj, proj, proj, proj)


def _outproj_kernel(oaf_ref, oab_ref, obf_ref, obb_ref, z_ref, gb_ref, x_ref, wa_ref, wb_ref, wo_ref,
                    o_ref, y_ref):
    width = oaf_ref.shape[1]
    for h in range(width // HEAD_DIM):
        sl = slice(h * HEAD_DIM, (h + 1) * HEAD_DIM)
        z = z_ref[:, sl]
        ya = _rms(oaf_ref[:, sl] + oab_ref[:, sl], wa_ref[...]) * (z * jax.nn.sigmoid(z))
        y_ref[:, sl] = _bf(ya)
        yb = _rms(obf_ref[:, sl] + obb_ref[:, sl], wb_ref[...]) * jax.nn.sigmoid(gb_ref[:, sl])
        y_ref[:, width + h * HEAD_DIM:width + (h + 1) * HEAD_DIM] = _bf(yb)
    o_ref[...] = x_ref[...] + _mm(y_ref[...], wo_ref[...])


def _out_proj(oaf, oab, obf, obb, proj, col_z, col_g, x2, gdn_w, hgrn_w, w_out):
    t, d = x2.shape
    wa = oaf.shape[1]
    tm = min(OUTPROJ_TM, t)
    row_blk = lambda c: pl.BlockSpec((tm, wa), lambda i: (i, c))
    full = lambda shape: pl.BlockSpec(shape, lambda i: (0, 0))
    return pl.pallas_call(
        _outproj_kernel,
        grid=(t // tm,),
        in_specs=[row_blk(0), row_blk(0), row_blk(0), row_blk(0),
                  row_blk(col_z // wa), row_blk(col_g // wa),
                  pl.BlockSpec((tm, d), lambda i: (i, 0)),
                  full((1, HEAD_DIM)), full((1, HEAD_DIM)), full(w_out.shape)],
        out_specs=pl.BlockSpec((tm, d), lambda i: (i, 0)),
        out_shape=jax.ShapeDtypeStruct((t, d), F32),
        scratch_shapes=[pltpu.VMEM((tm, w_out.shape[0]), BF16)],
        compiler_params=_params(("parallel",)),
        name="out_proj",
    )(oaf, oab, obf, obb, proj, proj, x2, gdn_w, hgrn_w, w_out)


def _ffn_kernel(x_ref, n2_ref, nf_ref, wg_ref, wu_ref, wd_ref, o_ref, h_ref, acc_ref):
    j = pl.program_id(1)

    @pl.when(j == 0)
    def _():
        h_ref[...] = _bf(_rms(x_ref[...], n2_ref[...]))
        acc_ref[...] = jnp.zeros_like(acc_ref)

    h = h_ref[...]
    gate = _mm(h, wg_ref[...])
    up = _mm(h, wu_ref[...])
    act = _bf(gate * jax.nn.sigmoid(gate) * up)
    acc_ref[...] += _mm(act, wd_ref[...])

    @pl.when(j == pl.num_programs(1) - 1)
    def _():
        o_ref[...] = _rms(x_ref[...] + acc_ref[...], nf_ref[...])


def _ffn(x1, norm2_w, normf_w, w_gate, w_up, w_down):
    t, d = x1.shape
    f = w_gate.shape[1]
    tm, tf = min(FFN_TM, t), FFN_TF
    return pl.pallas_call(
        _ffn_kernel,
        grid=(t // tm, f // tf),
        in_specs=[
            pl.BlockSpec((tm, d), lambda i, j: (i, 0)),
            pl.BlockSpec((1, d), lambda i, j: (0, 0)),
            pl.BlockSpec((1, d), lambda i, j: (0, 0)),
            pl.BlockSpec((d, tf), lambda i, j: (0, j)),
            pl.BlockSpec((d, tf), lambda i, j: (0, j)),
            pl.BlockSpec((tf, d), lambda i, j: (j, 0)),
        ],
        out_specs=pl.BlockSpec((tm, d), lambda i, j: (i, 0)),
        out_shape=jax.ShapeDtypeStruct((t, d), F32),
        scratch_shapes=[pltpu.VMEM((tm, d), BF16), pltpu.VMEM((tm, d), F32)],
        compiler_params=_params(("parallel", "arbitrary")),
        name="ffn",
    )(x1, norm2_w, normf_w, w_gate, w_up, w_down)


def kernel(x, norm1_w, w_in, conv_w, gdn_a_log, gdn_dt_bias, gdn_norm_w, hgrn_lb_logits, hgrn_norm_w,
           w_out, norm2_w, w_gate, w_up, w_down, norm_f_w):
    batch, seq, d = x.shape
    depth = w_in.shape[0]
    assert depth == 1 and seq % min(SCAN_ROWS, seq) == 0
    ga = GDN_HEADS * HEAD_DIM
    hb = HGRN_HEADS * HEAD_DIM
    n_gate = 4 * GDN_HEADS
    gate_lo = 4 * ga
    x2 = x.reshape(batch * seq, d)

    w = w_in[0]
    w_main = _bf(jnp.concatenate([w[:, :gate_lo], w[:, gate_lo + n_gate:]], axis=1))
    w_ba = _bf(jnp.pad(w[:, gate_lo:gate_lo + n_gate], ((0, 0), (0, HEAD_DIM - n_gate))))
    col_z = 3 * ga
    col_qb = 4 * ga
    col_ffw = col_qb + hb
    col_fbw = col_ffw + hb
    col_i = col_fbw + hb
    col_g = col_i + hb

    proj, ba = _in_proj(x2, norm1_w, w_main, w_ba)
    qkv = _conv_prep(proj, conv_w[0], seq)
    oaf, oab = _gdn_scan(qkv, ba, gdn_a_log[0], gdn_dt_bias[0], batch, seq)
    obf, obb = _hgrn_scan(proj, hgrn_lb_logits, col_qb, col_ffw, col_fbw, col_i, batch, seq)
    x1 = _out_proj(oaf, oab, obf, obb, proj, col_z, col_g, x2, gdn_norm_w, hgrn_norm_w, _bf(w_out[0]))
    out = _ffn(x1, norm2_w, norm_f_w.reshape(1, d), _bf(w_gate[0]), _bf(w_up[0]), _bf(w_down[0]))
    return out.reshape(batch, seq, d)
```

```python
import functools

import jax
import jax.numpy as jnp
from jax import lax
from jax.experimental import pallas as pl
from jax.experimental.pallas import tpu as pltpu

F32 = jnp.float32
BF16 = jnp.bfloat16

NORM_EPS = 1e-6
HEAD_DIM = 128
GDN_HEADS = 8
HGRN_HEADS = 8
CONV_K = 5
CONV_HALO = 8
SUBLANES = 8

SCAN_ROWS = 128
SCAN_HEADS = 4
GDN_STAGES_PER_HGRN_STAGE = 2
INPROJ_TM, INPROJ_TN = 1024, 512
CONV_TR, CONV_TC = 512, 512
OUTPROJ_TM = 256
FFN_TM, FFN_TF = 512, 512
VMEM_LIMIT = 56 * 1024 * 1024


def _params(sem):
    return pltpu.CompilerParams(dimension_semantics=sem, vmem_limit_bytes=VMEM_LIMIT)


def _bf(a):
    return a.astype(BF16)


def _mm(a, b):
    return jnp.dot(a, b, preferred_element_type=F32)


def _mm_nt(a, b):
    return lax.dot_general(a, b, (((1,), (1,)), ((), ())), preferred_element_type=F32)


def _mm_tn(a, b):
    return lax.dot_general(a, b, (((0,), (0,)), ((), ())), preferred_element_type=F32)


def _dot_01(m01_bf, x):
    hi = _bf(x)
    lo = _bf(x - hi.astype(F32))
    return _mm(m01_bf, hi) + _mm(m01_bf, lo)


def _rms(x, w):
    return x * lax.rsqrt(jnp.mean(x * x, axis=-1, keepdims=True) + NORM_EPS) * w


def _inproj_kernel(x_ref, nw_ref, w_ref, wba_ref, o_ref, ba_ref, h_ref):
    @pl.when(pl.program_id(1) == 0)
    def _():
        hb = _bf(_rms(x_ref[...], nw_ref[...]))
        h_ref[...] = hb
        ba_ref[...] = _mm(hb, wba_ref[...])

    o_ref[...] = _mm(h_ref[...], w_ref[...])


def _in_proj(x2, norm_w, w_main, w_ba):
    t, d = x2.shape
    n = w_main.shape[1]
    tm, tn = min(INPROJ_TM, t), INPROJ_TN
    return pl.pallas_call(
        _inproj_kernel,
        grid=(t // tm, n // tn),
        in_specs=[
            pl.BlockSpec((tm, d), lambda i, j: (i, 0)),
            pl.BlockSpec((1, d), lambda i, j: (0, 0)),
            pl.BlockSpec((d, tn), lambda i, j: (0, j)),
            pl.BlockSpec((d, HEAD_DIM), lambda i, j: (0, 0)),
        ],
        out_specs=[
            pl.BlockSpec((tm, tn), lambda i, j: (i, j)),
            pl.BlockSpec((tm, HEAD_DIM), lambda i, j: (i, 0)),
        ],
        out_shape=[jax.ShapeDtypeStruct((t, n), F32), jax.ShapeDtypeStruct((t, HEAD_DIM), F32)],
        scratch_shapes=[pltpu.VMEM((tm, d), BF16)],
        compiler_params=_params(("parallel", "arbitrary")),
        name="in_proj",
    )(x2, norm_w, w_main, w_ba)


def _conv_kernel(cur_ref, prev_ref, next_ref, w_ref, o_ref, ext_ref, *, blocks_per_seq, qk_blocks):
    i = pl.program_id(0)
    j = pl.program_id(1)
    tr = cur_ref.shape[0]
    pos = i % blocks_per_seq
    ext_ref[0:CONV_HALO, :] = jnp.where(pos == 0, 0.0, prev_ref[...])
    ext_ref[CONV_HALO:CONV_HALO + tr, :] = cur_ref[...]
    ext_ref[CONV_HALO + tr:, :] = jnp.where(pos == blocks_per_seq - 1, 0.0, next_ref[...])
    left = (CONV_K - 1) // 2
    acc = None
    for tap in range(CONV_K):
        start = CONV_HALO - left + tap
        term = w_ref[tap:tap + 1, :] * ext_ref[start:start + tr, :]
        acc = term if acc is None else acc + term
    act = acc * jax.nn.sigmoid(acc)
    is_q = j < qk_blocks
    is_qk = j < 2 * qk_blocks
    scale = jnp.where(is_q, HEAD_DIM ** -0.5, 1.0)
    for g in range(act.shape[1] // HEAD_DIM):
        seg = act[:, g * HEAD_DIM:(g + 1) * HEAD_DIM]
        normed = seg * lax.rsqrt(jnp.sum(seg * seg, axis=-1, keepdims=True) + NORM_EPS) * scale
        o_ref[:, g * HEAD_DIM:(g + 1) * HEAD_DIM] = jnp.where(is_qk, normed, seg)


def _conv_prep(proj, conv_w, seq):
    t = proj.shape[0]
    ch = conv_w.shape[1]
    tr, tc = min(CONV_TR, seq), CONV_TC
    halo_blocks = tr // CONV_HALO
    last_halo = t // CONV_HALO - 1
    kern = functools.partial(_conv_kernel, blocks_per_seq=seq // tr, qk_blocks=(ch // 3) // tc)
    return pl.pallas_call(
        kern,
        grid=(t // tr, ch // tc),
        in_specs=[
            pl.BlockSpec((tr, tc), lambda i, j: (i, j)),
            pl.BlockSpec((CONV_HALO, tc), lambda i, j: (jnp.maximum(i * halo_blocks - 1, 0), j)),
            pl.BlockSpec((CONV_HALO, tc), lambda i, j: (jnp.minimum((i + 1) * halo_blocks, last_halo), j)),
            pl.BlockSpec((CONV_K, tc), lambda i, j: (0, j)),
        ],
        out_specs=pl.BlockSpec((tr, tc), lambda i, j: (i, j)),
        out_shape=jax.ShapeDtypeStruct((t, ch), F32),
        scratch_shapes=[pltpu.VMEM((tr + 2 * CONV_HALO, tc), F32)],
        compiler_params=_params(("parallel", "parallel")),
        name="conv_prep",
    )(proj, proj, proj, conv_w)


def _head(ref, hh):
    return ref[:, hh * HEAD_DIM:(hh + 1) * HEAD_DIM]


def _row_blocks(m, blocks, s):
    return jnp.concatenate([m[b * s:(b + 1) * s] for b in blocks], axis=0)


def _unit_tri_inverses(chains, l_mats, n):
    row = lax.broadcasted_iota(jnp.int32, (n, n), 0)
    col = lax.broadcasted_iota(jnp.int32, (n, n), 1)
    xor_rc = row ^ col
    eye = jnp.where(xor_rc == 0, 1.0, 0.0)
    in_tile = xor_rc < SUBLANES
    lds = [jnp.where(in_tile, l, 0.0) for l in l_mats]
    invs = [eye - ld for ld in lds]
    powers = [_bf(ld) for ld in lds]
    for _ in range(2):
        powers = [_bf(_mm(p, p)) for p in powers]
        yield
        invs = [inv + _mm(_bf(inv), p) for inv, p in zip(invs, powers)]
        yield

    half_row = lax.broadcasted_iota(jnp.int32, (n // 2, n), 0)
    half_col = lax.broadcasted_iota(jnp.int32, (n // 2, n), 1)
    s = SUBLANES
    while s < n:
        blocks = range(n // s)
        pair2 = (half_row // s) * 2
        col_blk = half_col // s
        couples = {True: col_blk == pair2, False: col_blk == pair2 + 1}
        active = {True: [b for b in blocks if b % 2 == 1], False: [b for b in blocks if b % 2 == 0]}
        invs_bf = [_bf(inv) for inv in invs]
        xs = []
        for c, l, inv_bf in zip(chains, l_mats, invs_bf):
            off = jnp.where(couples[c["fwd"]], _row_blocks(l, active[c["fwd"]], s), 0.0)
            xs.append(_mm(_bf(off), inv_bf))
        yield
        new_invs = []
        for c, inv, x in zip(chains, invs, xs):
            act = active[c["fwd"]]
            zero = jnp.zeros((s, n), F32)
            x_full = jnp.concatenate([x[act.index(b) * s:(act.index(b) + 1) * s] if b in act else zero
                                      for b in blocks], axis=0)
            upd = _mm(_bf(_row_blocks(inv, act, s)), _bf(x_full))
            new_invs.append(jnp.concatenate(
                [inv[b * s:(b + 1) * s] - upd[act.index(b) * s:(act.index(b) + 1) * s] if b in act
                 else inv[b * s:(b + 1) * s] for b in blocks], axis=0))
        invs = new_invs
        yield
        s *= 2
    return invs


def _gdn_stages(chains, n, outs):
    row = lax.broadcasted_iota(jnp.int32, (n, n), 0)
    col = lax.broadcasted_iota(jnp.int32, (n, n), 1)
    earlier = {True: col < row, False: col > row}
    incl_bf = {d: _bf(jnp.where(earlier[d] | (row == col), 1.0, 0.0)) for d in (True, False)}
    ends = {True: (0, n - 1), False: (n - 1, 0)}

    lds = [_dot_01(incl_bf[c["fwd"]], jnp.where(earlier[c["fwd"]], c["g"], 0.0)) for c in chains]
    yield
    gcums, g_ends, decays, gammas = [], [], [], []
    for c, ld in zip(chains, lds):
        first, last = ends[c["fwd"]]
        gcum = ld[:, first:first + 1] + c["g"][first:first + 1, :]
        gcums.append(gcum)
        g_ends.append(gcum[last:last + 1, :])
        decays.append(jnp.exp(jnp.where(earlier[c["fwd"]] | (row == col), ld, -jnp.inf)))
        gammas.append(jnp.exp(gcum))

    k_bf = [_bf(c["k"]) for c in chains]
    qk_kk = [_mm_nt(_bf(jnp.concatenate([c["q"], c["k"]], axis=0)), kb) for c, kb in zip(chains, k_bf)]
    yield
    a_qk = [_bf(m[:n] * dec) for m, dec in zip(qk_kk, decays)]
    l_mats = [c["beta"] * m[n:] * jnp.where(earlier[c["fwd"]], dec, 0.0)
              for c, m, dec in zip(chains, qk_kk, decays)]
    t_invs = yield from _unit_tri_inverses(chains, l_mats, n)
    rhs = [_bf(jnp.concatenate([c["beta"] * c["v"], (c["beta"] * gam) * c["k"]], axis=1))
           for c, gam in zip(chains, gammas)]
    sols = [_mm(_bf(t), r) for t, r in zip(t_invs, rhs)]
    yield

    s0 = [c["s_ref"][...] for c in chains]
    s0_bf = [_bf(s) for s in s0]
    lhs = [_bf(jnp.concatenate([sol[:, HEAD_DIM:], c["q"] * gam], axis=0))
           for c, sol, gam in zip(chains, sols, gammas)]
    ws = [_mm(l, s) for l, s in zip(lhs, s0_bf)]
    yield
    w_bf = [_bf(sol[:, :HEAD_DIM] - x[:n]) for sol, x in zip(sols, ws)]
    outs.extend(x[n:] + _mm(a, w) for x, a, w in zip(ws, a_qk, w_bf))
    for c, s, w, ge, gc in zip(chains, s0, w_bf, g_ends, gcums):
        k_dec = _bf(c["k"] * jnp.exp(ge - gc))
        c["s_ref"][...] = jnp.exp(ge) * s + _mm_tn(k_dec, w)


def _block_ref_rows(bc, s, idx):
    n, d = bc.shape
    span = 2 * s
    if span >= SUBLANES:
        b3 = bc.reshape(n // span, span, d)
        return jnp.broadcast_to(b3[:, idx:idx + 1, :], b3.shape).reshape(n, d)
    b3 = bc.reshape(n // SUBLANES, SUBLANES, d)
    sub = lax.broadcasted_iota(jnp.int32, b3.shape, 1)
    out = None
    for g in reversed(range(SUBLANES // span)):
        piece = jnp.broadcast_to(b3[:, g * span + idx:g * span + idx + 1, :], b3.shape)
        out = piece if out is None else jnp.where(sub < (g + 1) * span, piece, out)
    return out.reshape(n, d)


def _hgrn_stages(chains, n, outs):
    tile = min(n, HEAD_DIM)
    n_tiles = n // tile
    assert n_tiles in (1, 2)
    row = lax.broadcasted_iota(jnp.int32, (n, n), 0)
    col = lax.broadcasted_iota(jnp.int32, (n, n), 1)
    incl_bf = {True: _bf(jnp.where(col <= row, 1.0, 0.0)), False: _bf(jnp.where(col >= row, 1.0, 0.0))}
    rowv = lax.broadcasted_iota(jnp.int32, (n, HEAD_DIM), 0)
    hrow = lax.broadcasted_iota(jnp.int32, (tile, tile), 0)
    hcol = lax.broadcasted_iota(jnp.int32, (tile, tile), 1)
    hxor = hrow ^ hcol
    keep = {True: hcol <= hrow, False: hcol >= hrow}

    bcs = [None] * len(chains)
    for d in (True, False):
        ids = [i for i, c in enumerate(chains) if c["fwd"] == d]
        if ids:
            cat = _dot_01(incl_bf[d], jnp.concatenate([chains[i]["lf"] for i in ids], axis=1))
            for pos, i in enumerate(ids):
                bcs[i] = cat[:, pos * HEAD_DIM:(pos + 1) * HEAD_DIM]
    yield

    diag = [[None] * n_tiles for _ in chains]
    cross = [None] * len(chains)
    s = 1
    while s < n:
        upper = (rowv % (2 * s)) >= s
        sign = {True: jnp.where(upper, 1.0, -1.0), False: jnp.where(upper, -1.0, 1.0)}
        ms = []
        for c, bc in zip(chains, bcs):
            q_side = upper if c["fwd"] else jnp.logical_not(upper)
            bref = _block_ref_rows(bc, s, s - 1 if c["fwd"] else s)
            e = jnp.exp((bc - bref) * sign[c["fwd"]])
            ms.append(_bf(jnp.where(q_side, c["q"], c["k"]) * e))
        if s < tile:
            level = (hxor // s) == 1
            for i, m in enumerate(ms):
                for t in range(n_tiles):
                    mt = m[t * tile:(t + 1) * tile]
                    prev = diag[i][t]
                    diag[i][t] = jnp.where(level, _mm_nt(mt, mt), 0.0 if prev is None else prev)
        else:
            for i, (c, m) in enumerate(zip(chains, ms)):
                cross[i] = _mm_nt(m[tile:], m[:tile]) if c["fwd"] else _mm_nt(m[:tile], m[tile:])
        yield
        s *= 2

    for i, (c, bc) in enumerate(zip(chains, bcs)):
        fwd = c["fwd"]
        qk_diag = jnp.sum(c["q"] * c["k"], axis=1, keepdims=True)
        v_bf = _bf(c["v"])
        o_tiles = []
        for t in range(n_tiles):
            a_full = jnp.where(hxor == 0, qk_diag[t * tile:(t + 1) * tile], diag[i][t])
            a_t = _bf(jnp.where(keep[fwd], a_full, 0.0))
            o_tiles.append(_mm(a_t, v_bf[t * tile:(t + 1) * tile]))
        if n_tiles == 2:
            if fwd:
                o_tiles[1] = o_tiles[1] + _mm(_bf(cross[i]), v_bf[:tile])
            else:
                o_tiles[0] = o_tiles[0] + _mm(_bf(cross[i]), v_bf[tile:])
        st0 = c["st_ref"][...]
        inter = _mm_nt(_bf(c["q"] * jnp.exp(bc)), _bf(st0))
        outs.append(inter + (o_tiles[0] if n_tiles == 1 else jnp.concatenate(o_tiles, axis=0)))
        last = n - 1 if fwd else 0
        b_end = bc[last:last + 1, :]
        c["st_ref"][...] = st0 * jnp.exp(b_end) + _mm_tn(v_bf, _bf(c["k"] * jnp.exp(b_end - bc)))


def _scan_kernel(alog_ref, dtb_ref, lg_ref,
                 gqf_ref, gkf_ref, gvf_ref, baf_ref, gqb_ref, gkb_ref, gvb_ref, bab_ref,
                 hqf_ref, hff_ref, hvf_ref, hqb_ref, hfb_ref, hvb_ref,
                 oaf_ref, oab_ref, obf_ref, obb_ref, s_ref, st_ref):
    hg = pl.program_id(1)
    n = gqf_ref.shape[0]

    @pl.when(pl.program_id(2) == 0)
    def _():
        s_ref[...] = jnp.zeros_like(s_ref)
        st_ref[...] = jnp.zeros_like(st_ref)

    lane = lax.broadcasted_iota(jnp.int32, baf_ref.shape, 1)
    gdn_chains = []
    for direction, (q_ref, k_ref, v_ref, ba_ref) in enumerate(
            ((gqf_ref, gkf_ref, gvf_ref, baf_ref), (gqb_ref, gkb_ref, gvb_ref, bab_ref))):
        ba = ba_ref[...]
        for hh in range(SCAN_HEADS):
            head = hg * SCAN_HEADS + hh
            pick = lambda idx: jnp.sum(jnp.where(lane == idx, ba, 0.0), axis=1, keepdims=True)
            beta = jax.nn.sigmoid(pick(direction * GDN_HEADS + head))
            a_raw = pick((2 + direction) * GDN_HEADS + head)
            g = (-jnp.exp(jnp.zeros_like(a_raw) + alog_ref[direction, head])
                 * jax.nn.softplus(a_raw + dtb_ref[direction, head]))
            gdn_chains.append(dict(q=_head(q_ref, hh), k=_head(k_ref, hh), v=_head(v_ref, hh), beta=beta, g=g,
                                   fwd=direction == 0, s_ref=s_ref.at[direction * SCAN_HEADS + hh]))

    lg = lg_ref[...]
    e = jnp.exp(lg - jnp.max(lg, axis=0, keepdims=True))
    lb_all = e[0] / jnp.sum(e, axis=0)
    hgrn_chains = []
    for direction, (q_ref, f_ref, v_ref) in enumerate(((hqf_ref, hff_ref, hvf_ref), (hqb_ref, hfb_ref, hvb_ref))):
        for hh in range(SCAN_HEADS):
            lb = lb_all[direction:direction + 1, hh * HEAD_DIM:(hh + 1) * HEAD_DIM]
            q_raw = _head(q_ref, hh)
            f = lb + (1.0 - lb) * jax.nn.sigmoid(_head(f_ref, hh))
            hgrn_chains.append(dict(q=q_raw * jax.nn.sigmoid(q_raw), k=1.0 - f, v=_head(v_ref, hh),
                                    lf=jnp.log(f), fwd=direction == 0,
                                    st_ref=st_ref.at[direction * SCAN_HEADS + hh]))

    gdn_out, hgrn_out = [], []
    gdn = _gdn_stages(gdn_chains, n, gdn_out)
    hgrn = _hgrn_stages(hgrn_chains, n, hgrn_out)
    live = [gdn, hgrn]
    while live:
        for gen, steps in ((gdn, GDN_STAGES_PER_HGRN_STAGE), (hgrn, 1)):
            for _ in range(steps):
                if gen in live and next(gen, StopIteration) is StopIteration:
                    live.remove(gen)

    for hh in range(SCAN_HEADS):
        sl = slice(hh * HEAD_DIM, (hh + 1) * HEAD_DIM)
        oaf_ref[:, sl] = gdn_out[hh]
        oab_ref[:, sl] = gdn_out[SCAN_HEADS + hh]
        obf_ref[:, sl] = hgrn_out[hh]
        obb_ref[:, sl] = hgrn_out[SCAN_HEADS + hh]


def _scan(qkv, ba, a_log, dt_bias, proj, lb_logits, col_q, col_ffw, col_fbw, col_i, batch, seq):
    t = qkv.shape[0]
    r = min(SCAN_ROWS, seq)
    nb = seq // r
    width = SCAN_HEADS * HEAD_DIM
    ga = GDN_HEADS * HEAD_DIM
    fwd_rows = lambda b, n: b * nb + n
    bwd_rows = lambda b, n: b * nb + (nb - 1 - n)

    def spec(rows, col_base):
        return pl.BlockSpec((r, width), lambda b, h, n: (rows(b, n), col_base // width + h))

    ba_spec = lambda rows: pl.BlockSpec((r, HEAD_DIM), lambda b, h, n: (rows(b, n), 0))
    smem = pl.BlockSpec(memory_space=pltpu.SMEM)
    layers = lb_logits.shape[0]
    out_sds = jax.ShapeDtypeStruct((t, ga), F32)
    state = pltpu.VMEM((2 * SCAN_HEADS, HEAD_DIM, HEAD_DIM), F32)
    return pl.pallas_call(
        _scan_kernel,
        grid=(batch, GDN_HEADS // SCAN_HEADS, nb),
        in_specs=[smem, smem, pl.BlockSpec((layers, 2, width), lambda b, h, n: (0, 0, h)),
                  spec(fwd_rows, 0), spec(fwd_rows, ga), spec(fwd_rows, 2 * ga), ba_spec(fwd_rows),
                  spec(bwd_rows, 0), spec(bwd_rows, ga), spec(bwd_rows, 2 * ga), ba_spec(bwd_rows),
                  spec(fwd_rows, col_q), spec(fwd_rows, col_ffw), spec(fwd_rows, col_i),
                  spec(bwd_rows, col_q), spec(bwd_rows, col_fbw), spec(bwd_rows, col_i)],
        out_specs=[spec(fwd_rows, 0), spec(bwd_rows, 0), spec(fwd_rows, 0), spec(bwd_rows, 0)],
        out_shape=[out_sds] * 4,
        scratch_shapes=[state, state],
        compiler_params=_params(("parallel", "parallel", "arbitrary")),
        name="scan",
    )(a_log, dt_bias, lb_logits, qkv, qkv, qkv, ba, qkv, qkv, qkv, ba, proj, proj, proj, proj, proj, proj)


def _outproj_kernel(oaf_ref, oab_ref, obf_ref, obb_ref, z_ref, gb_ref, x_ref, wa_ref, wb_ref, wo_ref,
                    o_ref, y_ref):
    width = oaf_ref.shape[1]
    for h in range(width // HEAD_DIM):
        sl = slice(h * HEAD_DIM, (h + 1) * HEAD_DIM)
        z = z_ref[:, sl]
        ya = _rms(oaf_ref[:, sl] + oab_ref[:, sl], wa_ref[...]) * (z * jax.nn.sigmoid(z))
        y_ref[:, sl] = _bf(ya)
        yb = _rms(obf_ref[:, sl] + obb_ref[:, sl], wb_ref[...]) * jax.nn.sigmoid(gb_ref[:, sl])
        y_ref[:, width + h * HEAD_DIM:width + (h + 1) * HEAD_DIM] = _bf(yb)
    o_ref[...] = x_ref[...] + _mm(y_ref[...], wo_ref[...])


def _out_proj(oaf, oab, obf, obb, proj, col_z, col_g, x2, gdn_w, hgrn_w, w_out):
    t, d = x2.shape
    wa = oaf.shape[1]
    tm = min(OUTPROJ_TM, t)
    row_blk = lambda c: pl.BlockSpec((tm, wa), lambda i: (i, c))
    full = lambda shape: pl.BlockSpec(shape, lambda i: (0, 0))
    return pl.pallas_call(
        _outproj_kernel,
        grid=(t // tm,),
        in_specs=[row_blk(0), row_blk(0), row_blk(0), row_blk(0),
                  row_blk(col_z // wa), row_blk(col_g // wa),
                  pl.BlockSpec((tm, d), lambda i: (i, 0)),
                  full((1, HEAD_DIM)), full((1, HEAD_DIM)), full(w_out.shape)],
        out_specs=pl.BlockSpec((tm, d), lambda i: (i, 0)),
        out_shape=jax.ShapeDtypeStruct((t, d), F32),
        scratch_shapes=[pltpu.VMEM((tm, w_out.shape[0]), BF16)],
        compiler_params=_params(("parallel",)),
        name="out_proj",
    )(oaf, oab, obf, obb, proj, proj, x2, gdn_w, hgrn_w, w_out)


def _ffn_kernel(x_ref, n2_ref, nf_ref, wg_ref, wu_ref, wd_ref, o_ref, h_ref, acc_ref):
    j = pl.program_id(1)

    @pl.when(j == 0)
    def _():
        h_ref[...] = _bf(_rms(x_ref[...], n2_ref[...]))
        acc_ref[...] = jnp.zeros_like(acc_ref)

    h = h_ref[...]
    gate = _mm(h, wg_ref[...])
    up = _mm(h, wu_ref[...])
    act = _bf(gate * jax.nn.sigmoid(gate) * up)
    acc_ref[...] += _mm(act, wd_ref[...])

    @pl.when(j == pl.num_programs(1) - 1)
    def _():
        o_ref[...] = _rms(x_ref[...] + acc_ref[...], nf_ref[...])


def _ffn(x1, norm2_w, normf_w, w_gate, w_up, w_down):
    t, d = x1.shape
    f = w_gate.shape[1]
    tm, tf = min(FFN_TM, t), FFN_TF
    return pl.pallas_call(
        _ffn_kernel,
        grid=(t // tm, f // tf),
        in_specs=[
            pl.BlockSpec((tm, d), lambda i, j: (i, 0)),
            pl.BlockSpec((1, d), lambda i, j: (0, 0)),
            pl.BlockSpec((1, d), lambda i, j: (0, 0)),
            pl.BlockSpec((d, tf), lambda i, j: (0, j)),
            pl.BlockSpec((d, tf), lambda i, j: (0, j)),
            pl.BlockSpec((tf, d), lambda i, j: (j, 0)),
        ],
        out_specs=pl.BlockSpec((tm, d), lambda i, j: (i, 0)),
        out_shape=jax.ShapeDtypeStruct((t, d), F32),
        scratch_shapes=[pltpu.VMEM((tm, d), BF16), pltpu.VMEM((tm, d), F32)],
        compiler_params=_params(("parallel", "arbitrary")),
        name="ffn",
    )(x1, norm2_w, normf_w, w_gate, w_up, w_down)


def kernel(x, norm1_w, w_in, conv_w, gdn_a_log, gdn_dt_bias, gdn_norm_w, hgrn_lb_logits, hgrn_norm_w,
           w_out, norm2_w, w_gate, w_up, w_down, norm_f_w):
    batch, seq, d = x.shape
    depth = w_in.shape[0]
    assert depth == 1 and seq % min(SCAN_ROWS, seq) == 0
    ga = GDN_HEADS * HEAD_DIM
    hb = HGRN_HEADS * HEAD_DIM
    n_gate = 4 * GDN_HEADS
    gate_lo = 4 * ga
    x2 = x.reshape(batch * seq, d)

    w = w_in[0]
    w_main = _bf(jnp.concatenate([w[:, :gate_lo], w[:, gate_lo + n_gate:]], axis=1))
    w_ba = _bf(jnp.pad(w[:, gate_lo:gate_lo + n_gate], ((0, 0), (0, HEAD_DIM - n_gate))))
    col_z = 3 * ga
    col_qb = 4 * ga
    col_ffw = col_qb + hb
    col_fbw = col_ffw + hb
    col_i = col_fbw + hb
    col_g = col_i + hb

    proj, ba = _in_proj(x2, norm1_w, w_main, w_ba)
    qkv = _conv_prep(proj, conv_w[0], seq)
    oaf, oab, obf, obb = _scan(qkv, ba, gdn_a_log[0], gdn_dt_bias[0], proj, hgrn_lb_logits,
                               col_qb, col_ffw, col_fbw, col_i, batch, seq)
    x1 = _out_proj(oaf, oab, obf, obb, proj, col_z, col_g, x2, gdn_norm_w, hgrn_norm_w, _bf(w_out[0]))
    out = _ffn(x1, norm2_w, norm_f_w.reshape(1, d), _bf(w_gate[0]), _bf(w_up[0]), _bf(w_down[0]))
    return out.reshape(batch, seq, d)
```
